```python
import math
import jax, jax.numpy as jnp
from jax import lax
import numpy as np

D_MODEL = 1024
BATCH = 2
SEQ = 8192
DEPTH = 2
DEC_BATCH = 128
DEC_SEQ = 8
PAST_LEN = 8192
PAGE_SIZE = 128

N_A = DEPTH // 2
N_B = DEPTH - N_A
POOL_WINDOWS = (2, 4, 8, 16)
POOL_GROUPS = 4
POOL_GC = D_MODEL // POOL_GROUPS
POOL_BUF = max(POOL_WINDOWS) - 1
N_HEADS = D_MODEL // 128
Q_RANK = 3 * D_MODEL // 8
KV_RANK = D_MODEL // 4
QK_NOPE = 128
QK_ROPE = 64
V_DIM = 128
SM_SCALE = 1.0 / math.sqrt(QK_NOPE + QK_ROPE)
ROPE_THETA = 10000.0
D_FF = 4 * D_MODEL
PLE_DIM = 256
Q_BLOCK = 128
NORM_EPS = 1e-6
NEG_INF = -1e30

kernel_name = 'yoco_pool_mla_decoder_step'


def _rmsnorm(x, g):
    xf = x.astype(jnp.float32)
    y = xf * lax.rsqrt(jnp.mean(xf * xf, axis=-1, keepdims=True) + NORM_EPS)
    return (y * g.astype(jnp.float32)).astype(x.dtype)


def _rope(x, pos):
    half = x.shape[-1] // 2
    inv_freq = jnp.power(ROPE_THETA, -jnp.arange(half, dtype=jnp.float32) / half)
    ang = pos.astype(jnp.float32)[:, None] * inv_freq[None, :]
    cos = jnp.cos(ang)[:, None, :]
    sin = jnp.sin(ang)[:, None, :]
    xf = x.astype(jnp.float32)
    x1, x2 = xf[..., :half], xf[..., half:]
    return jnp.concatenate([x1 * cos - x2 * sin, x2 * cos + x1 * sin], axis=-1).astype(x.dtype)


def _pool_mixer(u, buf, start_pos, w_grp, scale):
    T = u.shape[1]
    u_ext = jnp.concatenate([buf.astype(u.dtype), u], axis=1)
    cs = jnp.cumsum(u_ext.astype(jnp.float32), axis=1)
    cs = jnp.concatenate([jnp.zeros_like(cs[:, :1]), cs], axis=1)
    end = cs[:, POOL_BUF + 1:POOL_BUF + 1 + T]
    pos = start_pos + jnp.arange(T)
    uf = u.astype(jnp.float32)
    diffs = []
    for g, w in enumerate(POOL_WINDOWS):
        sl = slice(g * POOL_GC, (g + 1) * POOL_GC)
        win = end[..., sl] - cs[:, POOL_BUF + 1 - w:POOL_BUF + 1 - w + T, sl]
        cnt = jnp.minimum(pos + 1, w).astype(jnp.float32)[None, :, None]
        diffs.append(win / cnt - uf[..., sl])
    d = jnp.stack(diffs, axis=2).astype(u.dtype)
    mixed = jnp.einsum('btgc,gcd->btgd', d, w_grp).reshape(u.shape)
    return mixed * scale, u_ext[:, -POOL_BUF:]


def _shared_kv(h, pos, norm_kv, w_dkv, kv_norm):
    ckr = _rmsnorm(h, norm_kv) @ w_dkv
    c = _rmsnorm(ckr[..., :KV_RANK], kv_norm)
    kr = _rope(ckr[..., KV_RANK:][:, :, None, :], pos)[:, :, 0, :]
    return c, kr


def _mla_queries(u, pos, w_dq, q_norm, w_uq, w_uk):
    cq = _rmsnorm(u @ w_dq, q_norm)
    q = jnp.einsum('btr,rhe->bthe', cq, w_uq)
    q_lat = jnp.einsum('bthn,rhn->bthr', q[..., :QK_NOPE], w_uk)
    return q_lat, _rope(q[..., QK_NOPE:], pos)


def _latent_attend(q_lat, q_rope, parts):
    scores = []
    for c, kr, mask in parts:
        s = (jnp.einsum('bqhr,bkr->bhqk', q_lat, c)
             + jnp.einsum('bqhe,bke->bhqk', q_rope, kr)).astype(jnp.float32) * SM_SCALE
        if mask is not None:
            s = jnp.where(mask, s, NEG_INF)
        scores.append(s)
    probs = jax.nn.softmax(jnp.concatenate(scores, axis=-1), axis=-1)
    outs = []
    off = 0
    for c, _, _ in parts:
        n = c.shape[1]
        outs.append(jnp.einsum('bhqk,bkr->bqhr', probs[..., off:off + n].astype(c.dtype), c))
        off += n
    return sum(outs[1:], outs[0])


def _prompt_attention(q_lat, q_rope, c, kr):
    b, S = q_lat.shape[:2]
    nb = S // Q_BLOCK
    k_pos = jnp.arange(S)

    def blk(args):
        ql, qr, qpos = args
        mask = k_pos[None, :] <= qpos[:, None]
        return _latent_attend(ql, qr, [(c, kr, mask)])

    def split(a):
        return jnp.moveaxis(a.reshape((b, nb, Q_BLOCK) + a.shape[2:]), 1, 0)

    out = lax.map(blk, (split(q_lat), split(q_rope), k_pos.reshape(nb, Q_BLOCK)))
    return jnp.moveaxis(out, 0, 1).reshape((b, S) + out.shape[3:])


def _trunk(x, p, pool_state, past_c, past_kr, start_pos, w):
    T = x.shape[1]
    pos = start_pos + jnp.arange(T)
    h = x
    new_pool = []
    c = kr = None
    for i in range(DEPTH):
        u = _rmsnorm(h, w['norm_mix'][i])
        if i < N_A:
            mix, buf = _pool_mixer(u, pool_state[i], start_pos, w['pool_w'][i], w['pool_scale'][i])
            new_pool.append(buf)
        else:
            j = i - N_A
            q_lat, q_rope = _mla_queries(u, pos, w['w_dq'][j], w['q_norm'][j], w['w_uq'][j], w['w_uk'])
            if past_c is None:
                att = _prompt_attention(q_lat, q_rope, c, kr)
            else:
                causal = jnp.arange(T)[None, :] <= jnp.arange(T)[:, None]
                att = _latent_attend(q_lat, q_rope, [(past_c, past_kr, None), (c, kr, causal)])
            o = jnp.einsum('bqhr,rhv->bqhv', att, w['w_uv'])
            mix = jnp.einsum('bqhv,hvd->bqd', o, w['w_o'][j])
        h = h + mix
        a = jax.nn.relu(_rmsnorm(h, w['norm_mlp'][i]) @ w['w_up'][i])
        h = h + (a * a) @ w['w_down'][i]
        gate = jax.nn.sigmoid(_rmsnorm(h, w['norm_ple'][i]) @ w['w_ple_gate'][i])
        h = h + gate * (p[i] @ w['w_ple_proj'][i])
        if i == N_A - 1:
            c, kr = _shared_kv(h, pos, w['norm_kv'], w['w_dkv'], w['kv_norm'])
    return _rmsnorm(h, w['norm_final']), jnp.stack(new_pool), c, kr


def setup_inputs(seed: int = 0) -> dict:
    key = jax.random.key(seed)
    ks = iter(jax.random.split(key, 40))
    f32 = jnp.float32

    def nrm(shape, scale=1.0):
        return jax.random.normal(next(ks), shape, f32) * scale

    def gain(shape):
        return 1.0 + 0.1 * nrm(shape)

    n_pages = PAST_LEN // PAGE_SIZE
    n_pool = (DEC_BATCH * n_pages * 5) // 4
    page_table = jax.random.permutation(next(ks), n_pool)[:DEC_BATCH * n_pages]
    page_table = page_table.reshape(DEC_BATCH, n_pages).astype(jnp.int32)
    return {
        'x_prompt': nrm((BATCH, SEQ, D_MODEL)),
        'x_sample': nrm((DEC_BATCH, DEC_SEQ, D_MODEL)),
        'p_prompt': nrm((DEPTH, BATCH, SEQ, PLE_DIM)),
        'p_sample': nrm((DEPTH, DEC_BATCH, DEC_SEQ, PLE_DIM)),
        'state_pool': nrm((N_A, DEC_BATCH, POOL_BUF, D_MODEL)),
        'cache_latent': nrm((n_pool, PAGE_SIZE, KV_RANK)),
        'cache_krope': nrm((n_pool, PAGE_SIZE, QK_ROPE)),
        'page_table': page_table,
        'norm_mix': gain((DEPTH, D_MODEL)),
        'norm_mlp': gain((DEPTH, D_MODEL)),
        'norm_ple': gain((DEPTH, D_MODEL)),
        'pool_w': nrm((N_A, POOL_GROUPS, POOL_GC, POOL_GC), POOL_GC ** -0.5),
        'pool_scale': gain((N_A, D_MODEL)),
        'norm_kv': gain((D_MODEL,)),
        'w_dkv': nrm((D_MODEL, KV_RANK + QK_ROPE), D_MODEL ** -0.5),
        'kv_norm': gain((KV_RANK,)),
        'w_uk': nrm((KV_RANK, N_HEADS, QK_NOPE), KV_RANK ** -0.5),
        'w_uv': nrm((KV_RANK, N_HEADS, V_DIM), KV_RANK ** -0.5),
        'w_dq': nrm((N_B, D_MODEL, Q_RANK), D_MODEL ** -0.5),
        'q_norm': gain((N_B, Q_RANK)),
        'w_uq': nrm((N_B, Q_RANK, N_HEADS, QK_NOPE + QK_ROPE), Q_RANK ** -0.5),
        'w_o': nrm((N_B, N_HEADS, V_DIM, D_MODEL), (N_HEADS * V_DIM) ** -0.5),
        'w_up': nrm((DEPTH, D_MODEL, D_FF), D_MODEL ** -0.5),
        'w_down': nrm((DEPTH, D_FF, D_MODEL), D_FF ** -0.5),
        'w_ple_gate': nrm((DEPTH, D_MODEL, D_MODEL), D_MODEL ** -0.5),
        'w_ple_proj': nrm((DEPTH, PLE_DIM, D_MODEL), PLE_DIM ** -0.5),
        'norm_final': gain((D_MODEL,)),
    }


def reference(x_prompt, x_sample, p_prompt, p_sample, state_pool, cache_latent, cache_krope, page_table,
              norm_mix, norm_mlp, norm_ple, pool_w, pool_scale, norm_kv, w_dkv, kv_norm, w_uk, w_uv,
              w_dq, q_norm, w_uq, w_o, w_up, w_down, w_ple_gate, w_ple_proj, norm_final):
    w = dict(norm_mix=norm_mix, norm_mlp=norm_mlp, norm_ple=norm_ple, pool_w=pool_w,
             pool_scale=pool_scale, norm_kv=norm_kv, w_dkv=w_dkv, kv_norm=kv_norm, w_uk=w_uk,
             w_uv=w_uv, w_dq=w_dq, q_norm=q_norm, w_uq=w_uq, w_o=w_o, w_up=w_up, w_down=w_down,
             w_ple_gate=w_ple_gate, w_ple_proj=w_ple_proj, norm_final=norm_final)
    zero_pool = jnp.zeros((N_A, x_prompt.shape[0], POOL_BUF, D_MODEL), x_prompt.dtype)
    y_prompt, pool_prompt, latent_prompt, krope_prompt = _trunk(
        x_prompt, p_prompt, zero_pool, None, None, 0, w)
    b = page_table.shape[0]
    n_past = page_table.shape[1] * PAGE_SIZE
    past_c = cache_latent[page_table].reshape(b, n_past, KV_RANK)
    past_kr = cache_krope[page_table].reshape(b, n_past, QK_ROPE)
    y_sample, pool_sample, latent_sample, krope_sample = _trunk(
        x_sample, p_sample, state_pool, past_c, past_kr, PAST_LEN, w)
    return (y_prompt, y_sample, pool_prompt, pool_sample, latent_prompt, krope_prompt, latent_sample, krope_sample)
```

```python
import functools
import math

import jax
import jax.numpy as jnp
from jax import lax
from jax.experimental import pallas as pl
from jax.experimental.pallas import tpu as pltpu

D_MODEL = 1024
POOL_WINDOWS = (2, 4, 8, 16)
POOL_GC = D_MODEL // len(POOL_WINDOWS)
POOL_BUF = max(POOL_WINDOWS) - 1
POOL_HALO = POOL_BUF + 1
N_HEADS = 8
Q_RANK = 384
KV_RANK = 256
QK_NOPE = 128
QK_ROPE = 64
V_DIM = 128
D_FF = 4 * D_MODEL
PLE_DIM = 256
PAGE_SIZE = 128
ROPE_THETA = 10000.0
NORM_EPS = 1e-6
NEG_INF = -1e30
LANES = 128
Q_SCALE = (1.0 / math.sqrt(QK_NOPE + QK_ROPE)) * math.log2(math.e)
FF_CHUNK = 1024
VMEM_LIMIT = 56 * 1024 * 1024

BF16 = jnp.bfloat16
F32 = jnp.float32


def _rms(x, g):
    ms = jnp.mean(x * x, axis=-1, keepdims=True)
    return x * lax.rsqrt(ms + NORM_EPS) * g


def _dot(a, b):
    return jnp.dot(a, b, preferred_element_type=F32)


def _dot_nt(a, b):
    return lax.dot_general(a, b, (((1,), (1,)), ((), ())), preferred_element_type=F32)


def _mlp_ple(h, p, n_mlp, w_up, w_down, n_ple, w_gate, w_proj):
    un = _rms(h, n_mlp).astype(BF16)
    acc = h
    for j in range(D_FF // FF_CHUNK):
        a = jnp.maximum(_dot(un, w_up[:, j * FF_CHUNK:(j + 1) * FF_CHUNK]), 0.0)
        acc = acc + _dot((a * a).astype(BF16), w_down[j * FF_CHUNK:(j + 1) * FF_CHUNK, :])
    h = acc
    gate = jax.nn.sigmoid(_dot(_rms(h, n_ple).astype(BF16), w_gate[...]))
    return h + gate * _dot(p.astype(BF16), w_proj[...])


def _window_sums(ext, lo, hi):
    outs = []
    for g in range(len(POOL_WINDOWS)):
        s = ext[:, g * POOL_GC:(g + 1) * POOL_GC]
        for k in range(g + 1):
            s = s + pltpu.roll(s, 1 << k, axis=0)
        outs.append(s[lo:hi])
    return outs


def _mod(x, n):
    return x & (n - 1) if n & (n - 1) == 0 else x % n


def _front_kernel(sample, tm, start_pos, dec_seq, *refs):
    if sample:
        ext_ref, refs = refs[0], refs[1:]
    else:
        x_ref, halo_ref, refs = refs[0], refs[1], refs[2:]
    (p_ref, cos_ref, sin_ref, nmix0, nmlp0, nple0, pscale, pool_w, w_up, w_down, w_gate, w_proj,
     nkv, w_dkv, kvn, nmix1, w_dq, qn, w_qn, w_qr, w_qs, w_ukt,
     h_ref, u_ref, c_ref, kr_ref, ck_ref, krk_ref, ql_ref, qr_ref) = refs
    t = pl.program_id(1)

    if sample:
        seg = POOL_HALO + dec_seq
        nseg = tm // dec_seq
        ext_raw = ext_ref[0]
        row = lax.broadcasted_iota(jnp.int32, (nseg, seg, D_MODEL), 1).reshape(nseg * seg, D_MODEL)
        ext = jnp.where(row >= POOL_HALO, _rms(ext_raw, nmix0[...]), ext_raw)
        sums = _window_sums(ext, 0, nseg * seg)
        pick = lambda a: a.reshape(nseg, seg, a.shape[-1])[:, POOL_HALO:, :].reshape(tm, a.shape[-1])
        sums = [pick(s) for s in sums]
        x = pick(ext_raw)
        u = pick(ext)
        pos = start_pos + _mod(lax.broadcasted_iota(jnp.int32, (tm, 1), 0), dec_seq)
        u_ref[0] = u
    else:
        x = x_ref[0]
        u = _rms(x, nmix0[...])
        hist = _rms(halo_ref[0], nmix0[...]) * jnp.where(t > 0, 1.0, 0.0)
        ext = jnp.concatenate([hist, u], axis=0)
        sums = _window_sums(ext, POOL_HALO, POOL_HALO + tm)
        pos = start_pos + t * tm + lax.broadcasted_iota(jnp.int32, (tm, 1), 0)

        @pl.when(t == pl.num_programs(1) - 1)
        def _():
            u_ref[0] = u[tm - POOL_HALO:, :]

    mixed = []
    for g, w in enumerate(POOL_WINDOWS):
        cnt = jnp.minimum(pos + 1, w).astype(F32)
        d = sums[g] / cnt - u[:, g * POOL_GC:(g + 1) * POOL_GC]
        mixed.append(_dot(d.astype(BF16), pool_w[g]))
    h = x + jnp.concatenate(mixed, axis=-1) * pscale[...]
    h = _mlp_ple(h, p_ref[...], nmlp0[...], w_up, w_down, nple0[...], w_gate, w_proj)
    h_ref[0] = h

    cos = cos_ref[...]
    sin = sin_ref[...]

    ckr = _dot(_rms(h, nkv[...]).astype(BF16), w_dkv[...])
    c = _rms(ckr[:, :KV_RANK], kvn[...])
    kr = ckr[:, KV_RANK:KV_RANK + LANES] * cos + ckr[:, KV_RANK + LANES:] * sin
    c_ref[0] = c
    kr_ref[0] = kr[:, :QK_ROPE]
    ck_ref[0] = c.astype(ck_ref.dtype)
    krk_ref[0] = kr.astype(krk_ref.dtype)

    cq = _rms(_dot(_rms(h, nmix1[...]).astype(BF16), w_dq[...]), qn[...]).astype(BF16)
    q_nope = _dot(cq, w_qn[...]).astype(BF16)
    q_r = _dot(cq, w_qr[...])
    q_s = _dot(cq, w_qs[...])
    lane = lax.broadcasted_iota(jnp.int32, (1, LANES), 1)
    for hp in range(N_HEADS // 2):
        sl = slice(hp * LANES, (hp + 1) * LANES)
        rot = (q_r[:, sl] * cos + q_s[:, sl] * sin) * Q_SCALE
        qr_ref[0, 2 * hp] = jnp.where(lane < QK_ROPE, rot, 0.0).astype(qr_ref.dtype)
        qr_ref[0, 2 * hp + 1] = jnp.where(lane >= QK_ROPE, rot, 0.0).astype(qr_ref.dtype)
    for hd in range(N_HEADS):
        ql = _dot(q_nope[:, hd * QK_NOPE:(hd + 1) * QK_NOPE], w_ukt[hd]) * Q_SCALE
        ql_ref[0, hd] = ql.astype(ql_ref.dtype)


def _const_spec(shape):
    nd = len(shape)
    return pl.BlockSpec(shape, lambda *_: (0,) * nd, pipeline_mode=pl.Buffered(1))


def _front(sample, x_in, p_l, cos, sin, start_pos, wts, tm, n_b, n_t, dec_seq, act_dt):
    n_tok = n_t * tm
    if sample:
        seg = POOL_HALO + dec_seq
        data_specs = [pl.BlockSpec((1, tm // dec_seq * seg, D_MODEL), lambda b, t: (b, t, 0))]
        data = [x_in]
        u_shape, u_spec = (n_b, n_tok, D_MODEL), pl.BlockSpec((1, tm, D_MODEL), lambda b, t: (b, t, 0))
    else:
        hb = tm // POOL_HALO
        data_specs = [pl.BlockSpec((1, tm, D_MODEL), lambda b, t: (b, t, 0)),
                      pl.BlockSpec((1, POOL_HALO, D_MODEL), lambda b, t: (b, jnp.maximum(t * hb - 1, 0), 0))]
        data = [x_in, x_in]
        u_shape, u_spec = (n_b, POOL_HALO, D_MODEL), pl.BlockSpec((1, POOL_HALO, D_MODEL), lambda b, t: (b, 0, 0))
    data_specs += [pl.BlockSpec((None, tm, PLE_DIM), lambda b, t: (b, t, 0)),
                   pl.BlockSpec((tm, LANES), lambda b, t: (t, 0)),
                   pl.BlockSpec((tm, LANES), lambda b, t: (t, 0))]
    data += [p_l, cos, sin]
    tok = lambda width: pl.BlockSpec((1, tm, width), lambda b, t: (b, t, 0))
    head = lambda width: pl.BlockSpec((1, N_HEADS, tm, width), lambda b, t: (b, 0, t, 0))
    out_shape = [jax.ShapeDtypeStruct((n_b, n_tok, D_MODEL), F32),
                 jax.ShapeDtypeStruct(u_shape, F32),
                 jax.ShapeDtypeStruct((n_b, n_tok, KV_RANK), F32),
                 jax.ShapeDtypeStruct((n_b, n_tok, QK_ROPE), F32),
                 jax.ShapeDtypeStruct((n_b, n_tok, KV_RANK), act_dt),
                 jax.ShapeDtypeStruct((n_b, n_tok, LANES), act_dt),
                 jax.ShapeDtypeStruct((n_b, N_HEADS, n_tok, KV_RANK), act_dt),
                 jax.ShapeDtypeStruct((n_b, N_HEADS, n_tok, LANES), act_dt)]
    out_specs = [tok(D_MODEL), u_spec, tok(KV_RANK), tok(QK_ROPE), tok(KV_RANK), tok(LANES),
                 head(KV_RANK), head(LANES)]
    return pl.pallas_call(
        functools.partial(_front_kernel, sample, tm, start_pos, dec_seq),
        grid=(n_b, n_t),
        in_specs=data_specs + [_const_spec(w.shape) for w in wts],
        out_specs=out_specs,
        out_shape=out_shape,
        compiler_params=pltpu.CompilerParams(dimension_semantics=("arbitrary", "arbitrary"),
                                             vmem_limit_bytes=VMEM_LIMIT),
        name="front_sample" if sample else "front_prompt",
    )(*data, *wts)


def _back_kernel(h_ref, att_ref, p_ref, w_uv, w_o, nmlp, w_up, w_down, nple, w_gate, w_proj, nfin, y_ref):
    o = [_dot(att_ref[0, hd].astype(BF16), w_uv[hd]).astype(BF16) for hd in range(N_HEADS)]
    h = h_ref[0] + _dot(jnp.concatenate(o, axis=-1), w_o[...])
    h = _mlp_ple(h, p_ref[...], nmlp[...], w_up, w_down, nple[...], w_gate, w_proj)
    y_ref[0] = _rms(h, nfin[...])


def _back(h, att, p_l, wts, tm, name):
    n_b, n_tok, _ = h.shape
    return pl.pallas_call(
        _back_kernel,
        grid=(n_b, n_tok // tm),
        in_specs=[pl.BlockSpec((1, tm, D_MODEL), lambda b, t: (b, t, 0)),
                  pl.BlockSpec((1, N_HEADS, tm, KV_RANK), lambda b, t: (b, 0, t, 0)),
                  pl.BlockSpec((None, tm, PLE_DIM), lambda b, t: (b, t, 0))]
                 + [_const_spec(w.shape) for w in wts],
        out_specs=pl.BlockSpec((1, tm, D_MODEL), lambda b, t: (b, t, 0)),
        out_shape=jax.ShapeDtypeStruct((n_b, n_tok, D_MODEL), F32),
        compiler_params=pltpu.CompilerParams(dimension_semantics=("arbitrary", "arbitrary"),
                                             vmem_limit_bytes=VMEM_LIMIT),
        name=name,
    )(h, att, p_l, *wts)


def _softmax_step(s, v, m_sc, l_sc, acc_sc):
    reps = s.shape[1] // LANES
    m_prev = m_sc[...]
    m_new = jnp.maximum(m_prev, jnp.max(s, axis=1, keepdims=True))
    alpha = jnp.exp2(m_prev - m_new)
    p = jnp.exp2(s - jnp.concatenate([m_new] * reps, axis=1))
    part = p[:, :LANES]
    for r in range(1, reps):
        part = part + p[:, r * LANES:(r + 1) * LANES]
    l_sc[...] = alpha * l_sc[...] + part
    acc_sc[...] = acc_sc[...] * jnp.concatenate([alpha] * (KV_RANK // LANES), axis=1) + _dot(p.astype(BF16), v)
    m_sc[...] = m_new


def _prompt_attn_kernel(tq, ql_ref, qr_ref, c_ref, kr_ref, o_ref, m_sc, l_sc, acc_sc):
    i = pl.program_id(1)
    rows = N_HEADS * tq
    ql = ql_ref[0].reshape(rows, KV_RANK)
    qr = qr_ref[0].reshape(rows, LANES)
    m_sc[...] = jnp.full(m_sc.shape, -jnp.inf, F32)
    l_sc[...] = jnp.zeros(l_sc.shape, F32)
    acc_sc[...] = jnp.zeros(acc_sc.shape, F32)

    def scores(kc):
        k0 = pl.multiple_of(kc * tq, tq)
        ck = c_ref[pl.ds(k0, tq), :]
        return _dot_nt(ql, ck) + _dot_nt(qr, kr_ref[pl.ds(k0, tq), :]), ck

    def full_chunk(kc, carry):
        s, ck = scores(kc)
        _softmax_step(s, ck, m_sc, l_sc, acc_sc)
        return carry

    lax.fori_loop(0, i, full_chunk, 0)
    s, ck = scores(i)
    tok = _mod(lax.broadcasted_iota(jnp.int32, (rows, 1), 0), tq)
    key = lax.broadcasted_iota(jnp.int32, (1, tq), 1)
    _softmax_step(jnp.where(key <= tok, s, NEG_INF), ck, m_sc, l_sc, acc_sc)
    out = acc_sc[...] / jnp.sum(l_sc[...], axis=1, keepdims=True)
    o_ref[0] = out.reshape(N_HEADS, tq, KV_RANK).astype(o_ref.dtype)


def _prompt_attn(ql, qr, ck, krk, tq):
    n_b, _, seq, _ = ql.shape
    rows = N_HEADS * tq
    return pl.pallas_call(
        functools.partial(_prompt_attn_kernel, tq),
        grid=(n_b, seq // tq),
        in_specs=[pl.BlockSpec((1, N_HEADS, tq, KV_RANK), lambda b, i: (b, 0, i, 0)),
                  pl.BlockSpec((1, N_HEADS, tq, LANES), lambda b, i: (b, 0, i, 0)),
                  pl.BlockSpec((None, seq, KV_RANK), lambda b, i: (b, 0, 0)),
                  pl.BlockSpec((None, seq, LANES), lambda b, i: (b, 0, 0))],
        out_specs=pl.BlockSpec((1, N_HEADS, tq, KV_RANK), lambda b, i: (b, 0, i, 0)),
        out_shape=jax.ShapeDtypeStruct(ql.shape, BF16),
        scratch_shapes=[pltpu.VMEM((rows, LANES), F32), pltpu.VMEM((rows, LANES), F32),
                        pltpu.VMEM((rows, KV_RANK), F32)],
        compiler_params=pltpu.CompilerParams(dimension_semantics=("arbitrary", "arbitrary"),
                                             vmem_limit_bytes=VMEM_LIMIT),
        name="attend_prompt",
    )(ql, qr, ck, krk)


def _sample_attn_kernel(n_pages, dec_seq, pt_ref, ql_ref, qr_ref, cn_ref, krn_ref, cache_c, cache_kr, o_ref,
                        kc_buf, kr_buf, sem):
    b = pl.program_id(0)
    slot = b % 2
    n_past = n_pages * PAGE_SIZE
    rows = N_HEADS * dec_seq

    def page_copies(bb, sl, j):
        pg = pt_ref[bb, j]
        dst = pl.ds(pl.multiple_of(j * PAGE_SIZE, PAGE_SIZE), PAGE_SIZE)
        return (pltpu.make_async_copy(cache_c.at[pg], kc_buf.at[sl, dst], sem.at[0, sl]),
                pltpu.make_async_copy(cache_kr.at[pg], kr_buf.at[sl, dst], sem.at[1, sl]))

    def start_fetch(bb, sl):
        def body(j, carry):
            for cp in page_copies(bb, sl, j):
                cp.start()
            return carry
        lax.fori_loop(0, n_pages, body, 0)

    @pl.when(b == 0)
    def _():
        start_fetch(0, 0)

    @pl.when(b + 1 < pl.num_programs(0))
    def _():
        start_fetch(b + 1, 1 - slot)

    pad = jnp.zeros((PAGE_SIZE - dec_seq, KV_RANK), F32)
    kc_buf[slot, pl.ds(n_past, dec_seq), :] = cn_ref[0]
    kc_buf[slot, pl.ds(n_past + dec_seq, PAGE_SIZE - dec_seq), :] = pad
    kr_buf[slot, pl.ds(n_past, dec_seq), :] = krn_ref[0][:, :QK_ROPE]
    kr_buf[slot, pl.ds(n_past + dec_seq, PAGE_SIZE - dec_seq), :] = pad[:, :QK_ROPE]

    def wait_body(j, carry):
        for cp in page_copies(b, slot, j):
            cp.wait()
        return carry
    lax.fori_loop(0, n_pages, wait_body, 0)

    ql = ql_ref[...].reshape(rows, KV_RANK)
    qr2 = qr_ref[...].reshape(rows, LANES)
    qr = qr2[:, :QK_ROPE] + qr2[:, QK_ROPE:]
    kc = kc_buf[slot]
    s = _dot_nt(ql, kc) + _dot_nt(qr, kr_buf[slot])
    tok = _mod(lax.broadcasted_iota(jnp.int32, (rows, 1), 0), dec_seq)
    key = lax.broadcasted_iota(jnp.int32, (1, n_past + PAGE_SIZE), 1)
    s = jnp.where(key - n_past <= tok, s, NEG_INF)
    p = jnp.exp2(s - jnp.max(s, axis=1, keepdims=True))
    out = _dot(p, kc) / jnp.sum(p, axis=1, keepdims=True)
    o_ref[...] = out.reshape(N_HEADS, dec_seq, KV_RANK)


def _sample_attn(page_table, ql, qr, c_new, kr_new, cache_c, cache_kr):
    n_h, n_b, dec_seq, _ = ql.shape
    n_pages = page_table.shape[1]
    n_keys = (n_pages + 1) * PAGE_SIZE
    grid_spec = pltpu.PrefetchScalarGridSpec(
        num_scalar_prefetch=1,
        grid=(n_b,),
        in_specs=[pl.BlockSpec((n_h, None, dec_seq, KV_RANK), lambda b, pt: (0, b, 0, 0)),
                  pl.BlockSpec((n_h, None, dec_seq, LANES), lambda b, pt: (0, b, 0, 0)),
                  pl.BlockSpec((1, dec_seq, KV_RANK), lambda b, pt: (b, 0, 0)),
                  pl.BlockSpec((1, dec_seq, LANES), lambda b, pt: (b, 0, 0)),
                  pl.BlockSpec(memory_space=pl.ANY),
                  pl.BlockSpec(memory_space=pl.ANY)],
        out_specs=pl.BlockSpec((n_h, None, dec_seq, KV_RANK), lambda b, pt: (0, b, 0, 0)),
        scratch_shapes=[pltpu.VMEM((2, n_keys, KV_RANK), F32),
                        pltpu.VMEM((2, n_keys, QK_ROPE), F32),
                        pltpu.SemaphoreType.DMA((2, 2))])
    return pl.pallas_call(
        functools.partial(_sample_attn_kernel, n_pages, dec_seq),
        grid_spec=grid_spec,
        out_shape=jax.ShapeDtypeStruct(ql.shape, F32),
        compiler_params=pltpu.CompilerParams(dimension_semantics=("arbitrary",),
                                             vmem_limit_bytes=VMEM_LIMIT),
        name="attend_sample",
    )(page_table, ql, qr, c_new, kr_new, cache_c, cache_kr)


def _rope_tables(pos):
    half = QK_ROPE // 2
    inv_freq = jnp.power(ROPE_THETA, -jnp.arange(half, dtype=F32) / half)
    ang = pos.astype(F32)[:, None] * inv_freq[None, :]
    cos, sin = jnp.cos(ang), jnp.sin(ang)
    return jnp.concatenate([cos] * 4, axis=1), jnp.concatenate([-sin, sin] * 2, axis=1)


def _swap_halves(w):
    half = w.shape[-1] // 2
    return jnp.concatenate([w[..., half:], w[..., :half]], axis=-1)


def kernel(x_prompt, x_sample, p_prompt, p_sample, state_pool, cache_latent, cache_krope, page_table, norm_mix, norm_mlp, norm_ple, pool_w, pool_scale, norm_kv, w_dkv, kv_norm, w_uk, w_uv, w_dq, q_norm, w_uq, w_o, w_up, w_down, w_ple_gate, w_ple_proj, norm_final):
    n_b, seq, _ = x_prompt.shape
    dec_b, dec_seq, _ = x_sample.shape
    past_len = page_table.shape[1] * PAGE_SIZE
    row = lambda v: v.reshape(1, -1)
    bf = lambda w: w.astype(BF16)

    w_kr = w_dkv[:, KV_RANK:]
    w_dkv_x = bf(jnp.concatenate([w_dkv[:, :KV_RANK], w_kr, w_kr, _swap_halves(w_kr), _swap_halves(w_kr)], axis=1))
    w_qn = bf(w_uq[0, :, :, :QK_NOPE].reshape(Q_RANK, N_HEADS * QK_NOPE))
    w_qr = bf(w_uq[0, :, :, QK_NOPE:].reshape(Q_RANK, N_HEADS * QK_ROPE))
    w_qs = bf(_swap_halves(w_uq[0, :, :, QK_NOPE:]).reshape(Q_RANK, N_HEADS * QK_ROPE))
    w_ukt = bf(jnp.transpose(w_uk, (1, 2, 0)))
    front_w = [row(norm_mix[0]), row(norm_mlp[0]), row(norm_ple[0]), row(pool_scale[0]), bf(pool_w[0]),
               bf(w_up[0]), bf(w_down[0]), bf(w_ple_gate[0]), bf(w_ple_proj[0]),
               row(norm_kv), w_dkv_x, row(kv_norm), row(norm_mix[1]), bf(w_dq[0]), row(q_norm[0]),
               w_qn, w_qr, w_qs, w_ukt]
    back_w = [bf(jnp.transpose(w_uv, (1, 0, 2))), bf(w_o[0].reshape(N_HEADS * V_DIM, D_MODEL)),
              row(norm_mlp[1]), bf(w_up[1]), bf(w_down[1]), row(norm_ple[1]), bf(w_ple_gate[1]),
              bf(w_ple_proj[1]), row(norm_final)]

    tm = 512
    cos, sin = _rope_tables(jnp.arange(seq))
    h1, u_tail, lat_p, kr_p, ck, krk, ql, qr = _front(
        False, x_prompt, p_prompt[0], cos, sin, 0, front_w, tm, n_b, seq // tm, None, BF16)
    att = _prompt_attn(ql, qr, ck, krk, 256)
    y_prompt = _back(h1, att, p_prompt[1], back_w, tm, "back_prompt")
    pool_prompt = u_tail[None, :, 1:, :]

    n_tok = dec_b * dec_seq
    tms = 256
    ext = jnp.concatenate([jnp.zeros((dec_b, 1, D_MODEL), F32), state_pool[0], x_sample], axis=1)
    ext = ext.reshape(1, dec_b * (POOL_HALO + dec_seq), D_MODEL)
    cos_s, sin_s = _rope_tables(past_len + jnp.arange(n_tok) % dec_seq)
    h1s, u_s, lat_s, kr_s, cks, krks, qls, qrs = _front(
        True, ext, p_sample[0].reshape(1, n_tok, PLE_DIM), cos_s, sin_s, past_len, front_w, tms, 1,
        n_tok // tms, dec_seq, F32)
    att_s = _sample_attn(page_table,
                         qls.reshape(N_HEADS, dec_b, dec_seq, KV_RANK), qrs.reshape(N_HEADS, dec_b, dec_seq, LANES),
                         cks.reshape(dec_b, dec_seq, KV_RANK), krks.reshape(dec_b, dec_seq, LANES),
                         cache_latent, cache_krope)
    y_sample = _back(h1s, att_s.reshape(1, N_HEADS, n_tok, KV_RANK), p_sample[1].reshape(1, n_tok, PLE_DIM),
                     back_w, tms, "back_sample")
    pool_sample = jnp.concatenate([state_pool[0][:, dec_seq:, :], u_s.reshape(dec_b, dec_seq, D_MODEL)], axis=1)[None]

    return (y_prompt, y_sample.reshape(dec_b, dec_seq, D_MODEL), pool_prompt, pool_sample,
            lat_p, kr_p, lat_s.reshape(dec_b, dec_seq, KV_RANK), kr_s.reshape(dec_b, dec_seq, QK_ROPE))
```

```python
import functools
import math

import jax
import jax.numpy as jnp
from jax import lax
from jax.experimental import pallas as pl
from jax.experimental.pallas import tpu as pltpu

D_MODEL = 1024
POOL_WINDOWS = (2, 4, 8, 16)
POOL_GC = D_MODEL // len(POOL_WINDOWS)
POOL_BUF = max(POOL_WINDOWS) - 1
POOL_HALO = POOL_BUF + 1
N_HEADS = 8
Q_RANK = 384
KV_RANK = 256
QK_NOPE = 128
QK_ROPE = 64
V_DIM = 128
D_FF = 4 * D_MODEL
PLE_DIM = 256
PAGE_SIZE = 128
ROPE_THETA = 10000.0
NORM_EPS = 1e-6
NEG_INF = -1e30
LANES = 128
Q_SCALE = (1.0 / math.sqrt(QK_NOPE + QK_ROPE)) * math.log2(math.e)
FF_CHUNK = 1024
VMEM_LIMIT = 56 * 1024 * 1024

BF16 = jnp.bfloat16
F32 = jnp.float32


def _rms(x, g):
    ms = jnp.mean(x * x, axis=-1, keepdims=True)
    return x * lax.rsqrt(ms + NORM_EPS) * g


def _dot(a, b):
    return jnp.dot(a, b, preferred_element_type=F32)


def _dot_nt(a, b):
    return lax.dot_general(a, b, (((1,), (1,)), ((), ())), preferred_element_type=F32)


def _mlp_ple(h, p, n_mlp, w_up, w_down, n_ple, w_gate, w_proj):
    un = _rms(h, n_mlp).astype(BF16)
    acc = h
    for j in range(D_FF // FF_CHUNK):
        a = jnp.maximum(_dot(un, w_up[:, j * FF_CHUNK:(j + 1) * FF_CHUNK]), 0.0)
        acc = acc + _dot((a * a).astype(BF16), w_down[j * FF_CHUNK:(j + 1) * FF_CHUNK, :])
    h = acc
    gate = jax.nn.sigmoid(_dot(_rms(h, n_ple).astype(BF16), w_gate[...]))
    return h + gate * _dot(p.astype(BF16), w_proj[...])


def _window_sums(ext, lo, hi):
    outs = []
    for g in range(len(POOL_WINDOWS)):
        s = ext[:, g * POOL_GC:(g + 1) * POOL_GC]
        for k in range(g + 1):
            s = s + pltpu.roll(s, 1 << k, axis=0)
        outs.append(s[lo:hi])
    return outs


def _mod(x, n):
    return x & (n - 1) if n & (n - 1) == 0 else x % n


def _front_kernel(sample, tm, start_pos, dec_seq, *refs):
    if sample:
        ext_ref, refs = refs[0], refs[1:]
    else:
        x_ref, halo_ref, refs = refs[0], refs[1], refs[2:]
    (p_ref, cos_ref, sin_ref, nmix0, nmlp0, nple0, pscale, pool_w, w_up, w_down, w_gate, w_proj,
     nkv, w_dkv, kvn, nmix1, w_dq, qn, w_qn, w_qr, w_qs, w_ukt,
     h_ref, u_ref, c_ref, kr_ref, ck_ref, krk_ref, ql_ref, qr_ref) = refs
    t = pl.program_id(1)

    if sample:
        seg = POOL_HALO + dec_seq
        nseg = tm // dec_seq
        ext_raw = ext_ref[0]
        row = lax.broadcasted_iota(jnp.int32, (nseg, seg, D_MODEL), 1).reshape(nseg * seg, D_MODEL)
        ext = jnp.where(row >= POOL_HALO, _rms(ext_raw, nmix0[...]), ext_raw)
        sums = _window_sums(ext, 0, nseg * seg)
        pick = lambda a: a.reshape(nseg, seg, a.shape[-1])[:, POOL_HALO:, :].reshape(tm, a.shape[-1])
        sums = [pick(s) for s in sums]
        x = pick(ext_raw)
        u = pick(ext)
        pos = start_pos + _mod(lax.broadcasted_iota(jnp.int32, (tm, 1), 0), dec_seq)
        u_ref[0] = u
    else:
        x = x_ref[0]
        u = _rms(x, nmix0[...])
        hist = _rms(halo_ref[0], nmix0[...]) * jnp.where(t > 0, 1.0, 0.0)
        ext = jnp.concatenate([hist, u], axis=0)
        sums = _window_sums(ext, POOL_HALO, POOL_HALO + tm)
        pos = start_pos + t * tm + lax.broadcasted_iota(jnp.int32, (tm, 1), 0)

        @pl.when(t == pl.num_programs(1) - 1)
        def _():
            u_ref[0] = u[tm - POOL_HALO:, :]

    mixed = []
    for g, w in enumerate(POOL_WINDOWS):
        cnt = jnp.minimum(pos + 1, w).astype(F32)
        d = sums[g] / cnt - u[:, g * POOL_GC:(g + 1) * POOL_GC]
        mixed.append(_dot(d.astype(BF16), pool_w[g]))
    h = x + jnp.concatenate(mixed, axis=-1) * pscale[...]
    h = _mlp_ple(h, p_ref[...], nmlp0[...], w_up, w_down, nple0[...], w_gate, w_proj)
    h_ref[0] = h

    cos = cos_ref[...]
    sin = sin_ref[...]

    ckr = _dot(_rms(h, nkv[...]).astype(BF16), w_dkv[...])
    c = _rms(ckr[:, :KV_RANK], kvn[...])
    kr = ckr[:, KV_RANK:KV_RANK + LANES] * cos + ckr[:, KV_RANK + LANES:] * sin
    c_ref[0] = c
    kr_ref[0] = kr[:, :QK_ROPE]
    ck_ref[0] = c.astype(ck_ref.dtype)
    krk_ref[0] = kr.astype(krk_ref.dtype)

    cq = _rms(_dot(_rms(h, nmix1[...]).astype(BF16), w_dq[...]), qn[...]).astype(BF16)
    q_nope = _dot(cq, w_qn[...]).astype(BF16)
    q_r = _dot(cq, w_qr[...])
    q_s = _dot(cq, w_qs[...])
    lane = lax.broadcasted_iota(jnp.int32, (1, LANES), 1)
    for hp in range(N_HEADS // 2):
        sl = slice(hp * LANES, (hp + 1) * LANES)
        rot = (q_r[:, sl] * cos + q_s[:, sl] * sin) * Q_SCALE
        qr_ref[0, 2 * hp] = jnp.where(lane < QK_ROPE, rot, 0.0).astype(qr_ref.dtype)
        qr_ref[0, 2 * hp + 1] = jnp.where(lane >= QK_ROPE, rot, 0.0).astype(qr_ref.dtype)
    for hd in range(N_HEADS):
        ql = _dot(q_nope[:, hd * QK_NOPE:(hd + 1) * QK_NOPE], w_ukt[hd]) * Q_SCALE
        ql_ref[0, hd] = ql.astype(ql_ref.dtype)


def _const_spec(shape):
    nd = len(shape)
    return pl.BlockSpec(shape, lambda *_: (0,) * nd, pipeline_mode=pl.Buffered(1))


def _front(sample, x_in, p_l, cos, sin, start_pos, wts, tm, n_b, n_t, dec_seq, act_dt):
    n_tok = n_t * tm
    if sample:
        seg = POOL_HALO + dec_seq
        data_specs = [pl.BlockSpec((1, tm // dec_seq * seg, D_MODEL), lambda b, t: (b, t, 0))]
        data = [x_in]
        u_shape, u_spec = (n_b, n_tok, D_MODEL), pl.BlockSpec((1, tm, D_MODEL), lambda b, t: (b, t, 0))
    else:
        hb = tm // POOL_HALO
        data_specs = [pl.BlockSpec((1, tm, D_MODEL), lambda b, t: (b, t, 0)),
                      pl.BlockSpec((1, POOL_HALO, D_MODEL), lambda b, t: (b, jnp.maximum(t * hb - 1, 0), 0))]
        data = [x_in, x_in]
        u_shape, u_spec = (n_b, POOL_HALO, D_MODEL), pl.BlockSpec((1, POOL_HALO, D_MODEL), lambda b, t: (b, 0, 0))
    data_specs += [pl.BlockSpec((None, tm, PLE_DIM), lambda b, t: (b, t, 0)),
                   pl.BlockSpec((tm, LANES), lambda b, t: (t, 0)),
                   pl.BlockSpec((tm, LANES), lambda b, t: (t, 0))]
    data += [p_l, cos, sin]
    tok = lambda width: pl.BlockSpec((1, tm, width), lambda b, t: (b, t, 0))
    head = lambda width: pl.BlockSpec((1, N_HEADS, tm, width), lambda b, t: (b, 0, t, 0))
    out_shape = [jax.ShapeDtypeStruct((n_b, n_tok, D_MODEL), F32),
                 jax.ShapeDtypeStruct(u_shape, F32),
                 jax.ShapeDtypeStruct((n_b, n_tok, KV_RANK), F32),
                 jax.ShapeDtypeStruct((n_b, n_tok, QK_ROPE), F32),
                 jax.ShapeDtypeStruct((n_b, n_tok, KV_RANK), act_dt),
                 jax.ShapeDtypeStruct((n_b, n_tok, LANES), act_dt),
                 jax.ShapeDtypeStruct((n_b, N_HEADS, n_tok, KV_RANK), act_dt),
                 jax.ShapeDtypeStruct((n_b, N_HEADS, n_tok, LANES), act_dt)]
    out_specs = [tok(D_MODEL), u_spec, tok(KV_RANK), tok(QK_ROPE), tok(KV_RANK), tok(LANES),
                 head(KV_RANK), head(LANES)]
    return pl.pallas_call(
        functools.partial(_front_kernel, sample, tm, start_pos, dec_seq),
        grid=(n_b, n_t),
        in_specs=data_specs + [_const_spec(w.shape) for w in wts],
        out_specs=out_specs,
        out_shape=out_shape,
        compiler_params=pltpu.CompilerParams(dimension_semantics=("arbitrary", "arbitrary"),
                                             vmem_limit_bytes=VMEM_LIMIT),
        name="front_sample" if sample else "front_prompt",
    )(*data, *wts)


def _back_kernel(h_ref, att_ref, p_ref, w_uv, w_o, nmlp, w_up, w_down, nple, w_gate, w_proj, nfin, y_ref):
    o = [_dot(att_ref[0, hd].astype(BF16), w_uv[hd]).astype(BF16) for hd in range(N_HEADS)]
    h = h_ref[0] + _dot(jnp.concatenate(o, axis=-1), w_o[...])
    h = _mlp_ple(h, p_ref[...], nmlp[...], w_up, w_down, nple[...], w_gate, w_proj)
    y_ref[0] = _rms(h, nfin[...])


def _back(h, att, p_l, wts, tm, name):
    n_b, n_tok, _ = h.shape
    return pl.pallas_call(
        _back_kernel,
        grid=(n_b, n_tok // tm),
        in_specs=[pl.BlockSpec((1, tm, D_MODEL), lambda b, t: (b, t, 0)),
                  pl.BlockSpec((1, N_HEADS, tm, KV_RANK), lambda b, t: (b, 0, t, 0)),
                  pl.BlockSpec((None, tm, PLE_DIM), lambda b, t: (b, t, 0))]
                 + [_const_spec(w.shape) for w in wts],
        out_specs=pl.BlockSpec((1, tm, D_MODEL), lambda b, t: (b, t, 0)),
        out_shape=jax.ShapeDtypeStruct((n_b, n_tok, D_MODEL), F32),
        compiler_params=pltpu.CompilerParams(dimension_semantics=("arbitrary", "arbitrary"),
                                             vmem_limit_bytes=VMEM_LIMIT),
        name=name,
    )(h, att, p_l, *wts)


ATTN_PIECES = 8


def _softmax_step(s, v, sl, m_sc, l_sc, acc_sc):
    reps = s.shape[1] // LANES
    m_prev = m_sc[sl, :]
    m_new = jnp.maximum(m_prev, jnp.max(s, axis=1, keepdims=True))
    alpha = jnp.exp2(m_prev - m_new)
    p = jnp.exp2(s - jnp.concatenate([m_new] * reps, axis=1))
    part = p[:, :LANES]
    for r in range(1, reps):
        part = part + p[:, r * LANES:(r + 1) * LANES]
    l_sc[sl, :] = alpha * l_sc[sl, :] + part
    acc_sc[sl, :] = (acc_sc[sl, :] * jnp.concatenate([alpha] * (KV_RANK // LANES), axis=1)
                     + _dot(p.astype(BF16), v))
    m_sc[sl, :] = m_new


def _prompt_attn_kernel(tq, ql_ref, qr_ref, c_ref, kr_ref, o_ref, s_sc, m_sc, l_sc, acc_sc):
    i = pl.program_id(1)
    rows = N_HEADS * tq
    heads_pp = N_HEADS // ATTN_PIECES
    pr = heads_pp * tq
    pieces = [pl.ds(n * pr, pr) for n in range(ATTN_PIECES)]
    m_sc[...] = jnp.full(m_sc.shape, -jnp.inf, F32)
    l_sc[...] = jnp.zeros(l_sc.shape, F32)
    acc_sc[...] = jnp.zeros(acc_sc.shape, F32)

    def keys(kc):
        k0 = pl.multiple_of(kc * tq, tq)
        return c_ref[pl.ds(k0, tq), :], kr_ref[pl.ds(k0, tq), :]

    def scores(n, ck, krk):
        hs = slice(n * heads_pp, (n + 1) * heads_pp)
        ql = ql_ref[0, hs].reshape(pr, KV_RANK)
        qr = qr_ref[0, hs].reshape(pr, LANES)
        return _dot_nt(ql, ck) + _dot_nt(qr, krk)

    ck, krk = keys(0)
    for n, sl in enumerate(pieces):
        s_sc[sl, :] = scores(n, ck, krk)

    def full_chunk(kc, carry):
        ck, _ = keys(kc)
        ck_next, krk_next = keys(kc + 1)
        for n, sl in enumerate(pieces):
            s = s_sc[sl, :]
            s_sc[sl, :] = scores(n, ck_next, krk_next)
            _softmax_step(s, ck, sl, m_sc, l_sc, acc_sc)
        return carry

    lax.fori_loop(0, i, full_chunk, 0)
    ck, _ = keys(i)
    tok = _mod(lax.broadcasted_iota(jnp.int32, (pr, 1), 0), tq)
    key = lax.broadcasted_iota(jnp.int32, (1, tq), 1)
    for sl in pieces:
        _softmax_step(jnp.where(key <= tok, s_sc[sl, :], NEG_INF), ck, sl, m_sc, l_sc, acc_sc)
    out = acc_sc[...] / jnp.sum(l_sc[...], axis=1, keepdims=True)
    o_ref[0] = out.reshape(N_HEADS, tq, KV_RANK).astype(o_ref.dtype)


def _prompt_attn(ql, qr, ck, krk, tq):
    n_b, _, seq, _ = ql.shape
    rows = N_HEADS * tq
    return pl.pallas_call(
        functools.partial(_prompt_attn_kernel, tq),
        grid=(n_b, seq // tq),
        in_specs=[pl.BlockSpec((1, N_HEADS, tq, KV_RANK), lambda b, i: (b, 0, i, 0)),
                  pl.BlockSpec((1, N_HEADS, tq, LANES), lambda b, i: (b, 0, i, 0)),
                  pl.BlockSpec((None, seq, KV_RANK), lambda b, i: (b, 0, 0)),
                  pl.BlockSpec((None, seq, LANES), lambda b, i: (b, 0, 0))],
        out_specs=pl.BlockSpec((1, N_HEADS, tq, KV_RANK), lambda b, i: (b, 0, i, 0)),
        out_shape=jax.ShapeDtypeStruct(ql.shape, BF16),
        scratch_shapes=[pltpu.VMEM((rows, tq), F32), pltpu.VMEM((rows, LANES), F32),
                        pltpu.VMEM((rows, LANES), F32), pltpu.VMEM((rows, KV_RANK), F32)],
        compiler_params=pltpu.CompilerParams(dimension_semantics=("arbitrary", "arbitrary"),
                                             vmem_limit_bytes=VMEM_LIMIT),
        name="attend_prompt",
    )(ql, qr, ck, krk)


SAMPLE_CHUNK_PAGES = 16


def _sample_attn_kernel(n_pages, dec_seq, pt_ref, ql_ref, qr_ref, cn_ref, krn_ref, cache_c, cache_krt, o_ref,
                        kc_buf, krt_buf, sem):
    b = pl.program_id(0)
    slot = b % 2
    rows = N_HEADS * dec_seq

    def page_copies(bb, sl, j):
        pg = pt_ref[bb, j]
        return (pltpu.make_async_copy(cache_c.at[pg], kc_buf.at[sl, j], sem.at[0, sl]),
                pltpu.make_async_copy(cache_krt.at[pg], krt_buf.at[sl, :, pl.ds(j * PAGE_SIZE, PAGE_SIZE)],
                                      sem.at[1, sl]))

    def start_fetch(bb, sl):
        for j in range(n_pages):
            for cp in page_copies(bb, sl, j):
                cp.start()

    @pl.when(b == 0)
    def _():
        start_fetch(0, 0)

    @pl.when(b + 1 < pl.num_programs(0))
    def _():
        start_fetch(b + 1, 1 - slot)

    for j in range(n_pages):
        for cp in page_copies(b, slot, j):
            cp.wait()

    ql = ql_ref[...].reshape(rows, KV_RANK)
    qr2 = qr_ref[...].reshape(rows, LANES)
    qr = qr2[:, :QK_ROPE] + qr2[:, QK_ROPE:]

    def block(s, v):
        m = jnp.max(s, axis=1, keepdims=True)
        p = jnp.exp2(s - m)
        return m, jnp.sum(p, axis=1, keepdims=True), _dot(p, v)

    ck_keys = SAMPLE_CHUNK_PAGES * PAGE_SIZE
    n_chunks = n_pages // SAMPLE_CHUNK_PAGES
    values = [kc_buf[slot, pl.ds(j * SAMPLE_CHUNK_PAGES, SAMPLE_CHUNK_PAGES)].reshape(ck_keys, KV_RANK)
              for j in range(n_chunks)]
    scores = [_dot_nt(ql, values[j]) + _dot(qr, krt_buf[slot, :, pl.ds(j * ck_keys, ck_keys)])
              for j in range(n_chunks)]
    parts = [block(s, v) for s, v in zip(scores, values)]
    cn = jnp.concatenate([cn_ref[0], jnp.zeros((PAGE_SIZE - dec_seq, KV_RANK), F32)], axis=0)
    krn = jnp.concatenate([krn_ref[0], jnp.zeros((PAGE_SIZE - dec_seq, LANES), F32)], axis=0)
    s = _dot_nt(ql, cn) + _dot_nt(qr2, krn)
    tok = _mod(lax.broadcasted_iota(jnp.int32, (rows, 1), 0), dec_seq)
    key = lax.broadcasted_iota(jnp.int32, (1, PAGE_SIZE), 1)
    parts.append(block(jnp.where(key <= tok, s, NEG_INF), cn))

    m_all = functools.reduce(jnp.maximum, [m for m, _, _ in parts])
    num = jnp.zeros((rows, KV_RANK), F32)
    den = jnp.zeros((rows, 1), F32)
    for m, l, o in parts:
        w = jnp.exp2(m - m_all)
        num = num + w * o
        den = den + w * l
    o_ref[...] = (num / den).reshape(N_HEADS, dec_seq, KV_RANK)


def _sample_attn(page_table, ql, qr, c_new, kr_new, cache_c, cache_krt):
    n_h, n_b, dec_seq, _ = ql.shape
    n_pages = page_table.shape[1]
    assert n_pages % SAMPLE_CHUNK_PAGES == 0
    grid_spec = pltpu.PrefetchScalarGridSpec(
        num_scalar_prefetch=1,
        grid=(n_b,),
        in_specs=[pl.BlockSpec((n_h, None, dec_seq, KV_RANK), lambda b, pt: (0, b, 0, 0)),
                  pl.BlockSpec((n_h, None, dec_seq, LANES), lambda b, pt: (0, b, 0, 0)),
                  pl.BlockSpec((1, dec_seq, KV_RANK), lambda b, pt: (b, 0, 0)),
                  pl.BlockSpec((1, dec_seq, LANES), lambda b, pt: (b, 0, 0)),
                  pl.BlockSpec(memory_space=pl.ANY),
                  pl.BlockSpec(memory_space=pl.ANY)],
        out_specs=pl.BlockSpec((n_h, None, dec_seq, KV_RANK), lambda b, pt: (0, b, 0, 0)),
        scratch_shapes=[pltpu.VMEM((2, n_pages, PAGE_SIZE, KV_RANK), F32),
                        pltpu.VMEM((2, QK_ROPE, n_pages * PAGE_SIZE), F32),
                        pltpu.SemaphoreType.DMA((2, 2))])
    return pl.pallas_call(
        functools.partial(_sample_attn_kernel, n_pages, dec_seq),
        grid_spec=grid_spec,
        out_shape=jax.ShapeDtypeStruct(ql.shape, F32),
        compiler_params=pltpu.CompilerParams(dimension_semantics=("arbitrary",),
                                             vmem_limit_bytes=VMEM_LIMIT),
        name="attend_sample",
    )(page_table, ql, qr, c_new, kr_new, cache_c, cache_krt)


def _rope_tables(pos):
    half = QK_ROPE // 2
    inv_freq = jnp.power(ROPE_THETA, -jnp.arange(half, dtype=F32) / half)
    ang = pos.astype(F32)[:, None] * inv_freq[None, :]
    cos, sin = jnp.cos(ang), jnp.sin(ang)
    return jnp.concatenate([cos] * 4, axis=1), jnp.concatenate([-sin, sin] * 2, axis=1)


def _swap_halves(w):
    half = w.shape[-1] // 2
    return jnp.concatenate([w[..., half:], w[..., :half]], axis=-1)


def kernel(x_prompt, x_sample, p_prompt, p_sample, state_pool, cache_latent, cache_krope, page_table, norm_mix, norm_mlp, norm_ple, pool_w, pool_scale, norm_kv, w_dkv, kv_norm, w_uk, w_uv, w_dq, q_norm, w_uq, w_o, w_up, w_down, w_ple_gate, w_ple_proj, norm_final):
    n_b, seq, _ = x_prompt.shape
    dec_b, dec_seq, _ = x_sample.shape
    past_len = page_table.shape[1] * PAGE_SIZE
    row = lambda v: v.reshape(1, -1)
    bf = lambda w: w.astype(BF16)

    w_kr = w_dkv[:, KV_RANK:]
    w_dkv_x = bf(jnp.concatenate([w_dkv[:, :KV_RANK], w_kr, w_kr, _swap_halves(w_kr), _swap_halves(w_kr)], axis=1))
    w_qn = bf(w_uq[0, :, :, :QK_NOPE].reshape(Q_RANK, N_HEADS * QK_NOPE))
    w_qr = bf(w_uq[0, :, :, QK_NOPE:].reshape(Q_RANK, N_HEADS * QK_ROPE))
    w_qs = bf(_swap_halves(w_uq[0, :, :, QK_NOPE:]).reshape(Q_RANK, N_HEADS * QK_ROPE))
    w_ukt = bf(jnp.transpose(w_uk, (1, 2, 0)))
    front_w = [row(norm_mix[0]), row(norm_mlp[0]), row(norm_ple[0]), row(pool_scale[0]), bf(pool_w[0]),
               bf(w_up[0]), bf(w_down[0]), bf(w_ple_gate[0]), bf(w_ple_proj[0]),
               row(norm_kv), w_dkv_x, row(kv_norm), row(norm_mix[1]), bf(w_dq[0]), row(q_norm[0]),
               w_qn, w_qr, w_qs, w_ukt]
    back_w = [bf(jnp.transpose(w_uv, (1, 0, 2))), bf(w_o[0].reshape(N_HEADS * V_DIM, D_MODEL)),
              row(norm_mlp[1]), bf(w_up[1]), bf(w_down[1]), row(norm_ple[1]), bf(w_ple_gate[1]),
              bf(w_ple_proj[1]), row(norm_final)]

    tm = 512
    cos, sin = _rope_tables(jnp.arange(seq))
    h1, u_tail, lat_p, kr_p, ck, krk, ql, qr = _front(
        False, x_prompt, p_prompt[0], cos, sin, 0, front_w, tm, n_b, seq // tm, None, BF16)
    att = _prompt_attn(ql, qr, ck, krk, 512)
    y_prompt = _back(h1, att, p_prompt[1], back_w, tm, "back_prompt")
    pool_prompt = u_tail[None, :, 1:, :]

    n_tok = dec_b * dec_seq
    tms = 256
    ext = jnp.concatenate([jnp.zeros((dec_b, 1, D_MODEL), F32), state_pool[0], x_sample], axis=1)
    ext = ext.reshape(1, dec_b * (POOL_HALO + dec_seq), D_MODEL)
    cos_s, sin_s = _rope_tables(past_len + jnp.arange(n_tok) % dec_seq)
    h1s, u_s, lat_s, kr_s, cks, krks, qls, qrs = _front(
        True, ext, p_sample[0].reshape(1, n_tok, PLE_DIM), cos_s, sin_s, past_len, front_w, tms, 1,
        n_tok // tms, dec_seq, F32)
    att_s = _sample_attn(page_table,
                         qls.reshape(N_HEADS, dec_b, dec_seq, KV_RANK), qrs.reshape(N_HEADS, dec_b, dec_seq, LANES),
                         cks.reshape(dec_b, dec_seq, KV_RANK), krks.reshape(dec_b, dec_seq, LANES),
                         cache_latent, jnp.transpose(cache_krope, (0, 2, 1)))
    y_sample = _back(h1s, att_s.reshape(1, N_HEADS, n_tok, KV_RANK), p_sample[1].reshape(1, n_tok, PLE_DIM),
                     back_w, tms, "back_sample")
    pool_sample = jnp.concatenate([state_pool[0][:, dec_seq:, :], u_s.reshape(dec_b, dec_seq, D_MODEL)], axis=1)[None]

    return (y_prompt, y_sample.reshape(dec_b, dec_seq, D_MODEL), pool_prompt, pool_sample,
            lat_p, kr_p, lat_s.reshape(dec_b, dec_seq, KV_RANK), kr_s.reshape(dec_b, dec_seq, QK_ROPE))
```

```python
import functools
import math

import jax
import jax.numpy as jnp
from jax import lax
from jax.experimental import pallas as pl
from jax.experimental.pallas import tpu as pltpu

D_MODEL = 1024
POOL_WINDOWS = (2, 4, 8, 16)
POOL_GC = D_MODEL // len(POOL_WINDOWS)
POOL_BUF = max(POOL_WINDOWS) - 1
POOL_HALO = POOL_BUF + 1
N_HEADS = 8
Q_RANK = 384
KV_RANK = 256
QK_NOPE = 128
QK_ROPE = 64
V_DIM = 128
D_FF = 4 * D_MODEL
PLE_DIM = 256
PAGE_SIZE = 128
ROPE_THETA = 10000.0
NORM_EPS = 1e-6
NEG_INF = -1e30
LANES = 128
Q_SCALE = (1.0 / math.sqrt(QK_NOPE + QK_ROPE)) * math.log2(math.e)
FF_CHUNK = 1024
VMEM_LIMIT = 56 * 1024 * 1024

BF16 = jnp.bfloat16
F32 = jnp.float32


def _rms(x, g):
    ms = jnp.mean(x * x, axis=-1, keepdims=True)
    return x * lax.rsqrt(ms + NORM_EPS) * g


def _dot(a, b):
    return jnp.dot(a, b, preferred_element_type=F32)


def _dot_nt(a, b):
    return lax.dot_general(a, b, (((1,), (1,)), ((), ())), preferred_element_type=F32)


ROW_SPLIT = 2


def _mlp_ple(hs, ps, n_mlp, w_up, w_down, n_ple, w_gate, w_proj):
    un = [_rms(h, n_mlp).astype(BF16) for h in hs]
    for j in range(D_FF // FF_CHUNK):
        a = [jnp.maximum(_dot(u, w_up[:, j * FF_CHUNK:(j + 1) * FF_CHUNK]), 0.0) for u in un]
        hs = [h + _dot((x * x).astype(BF16), w_down[j * FF_CHUNK:(j + 1) * FF_CHUNK, :]) for h, x in zip(hs, a)]
    gate = [jax.nn.sigmoid(_dot(_rms(h, n_ple).astype(BF16), w_gate[...])) for h in hs]
    return [h + g * _dot(p.astype(BF16), w_proj[...]) for h, g, p in zip(hs, gate, ps)]


def _window_sums(ext, lo, hi):
    outs = []
    for g in range(len(POOL_WINDOWS)):
        s = ext[:, g * POOL_GC:(g + 1) * POOL_GC]
        for k in range(g + 1):
            s = s + pltpu.roll(s, 1 << k, axis=0)
        outs.append(s[lo:hi])
    return outs


def _mod(x, n):
    return x & (n - 1) if n & (n - 1) == 0 else x % n


def _front_kernel(sample, tm, start_pos, dec_seq, *refs):
    if sample:
        ext_ref, refs = refs[0], refs[1:]
    else:
        x_ref, halo_ref, refs = refs[0], refs[1], refs[2:]
    (p_ref, cosa_ref, sina_ref, cosb_ref, sinb_ref,
     nmix0, nmlp0, nple0, pscale, pool_w, w_up, w_down, w_gate, w_proj,
     nkv, w_dkv, kvn, nmix1, w_dq, qn, w_qn, w_qr, w_qs, w_ukt,
     h_ref, u_ref, c_ref, kr_ref, ck_ref, krk_ref, ql_ref, qr_ref) = refs
    t = pl.program_id(1)

    if sample:
        seg = POOL_HALO + dec_seq
        nseg = tm // dec_seq
        ext_raw = ext_ref[0]
        row = lax.broadcasted_iota(jnp.int32, (nseg, seg, D_MODEL), 1).reshape(nseg * seg, D_MODEL)
        ext = jnp.where(row >= POOL_HALO, _rms(ext_raw, nmix0[...]), ext_raw)
        sums = _window_sums(ext, 0, nseg * seg)
        pick = lambda a: a.reshape(nseg, seg, a.shape[-1])[:, POOL_HALO:, :].reshape(tm, a.shape[-1])
        sums = [pick(s) for s in sums]
        x = pick(ext_raw)
        u = pick(ext)
        pos = start_pos + _mod(lax.broadcasted_iota(jnp.int32, (tm, 1), 0), dec_seq)
        u_ref[0] = u
    else:
        x = x_ref[0]
        u = _rms(x, nmix0[...])
        hist = _rms(halo_ref[0], nmix0[...]) * jnp.where(t > 0, 1.0, 0.0)
        ext = jnp.concatenate([hist, u], axis=0)
        sums = _window_sums(ext, POOL_HALO, POOL_HALO + tm)
        pos = start_pos + t * tm + lax.broadcasted_iota(jnp.int32, (tm, 1), 0)

        @pl.when(t == pl.num_programs(1) - 1)
        def _():
            u_ref[0] = u[tm - POOL_HALO:, :]

    hr = tm // ROW_SPLIT
    groups = [slice(k * hr, (k + 1) * hr) for k in range(ROW_SPLIT)]

    hs = []
    for r in groups:
        mixed = []
        for g, w in enumerate(POOL_WINDOWS):
            cnt = jnp.minimum(pos[r] + 1, w).astype(F32)
            d = sums[g][r] / cnt - u[r, g * POOL_GC:(g + 1) * POOL_GC]
            mixed.append(_dot(d.astype(BF16), pool_w[g]))
        hs.append(x[r] + jnp.concatenate(mixed, axis=-1) * pscale[...])
    hs = _mlp_ple(hs, [p_ref[r, :] for r in groups], nmlp0[...], w_up, w_down, nple0[...], w_gate, w_proj)
    for r, h in zip(groups, hs):
        h_ref[0, r, :] = h

    lane = lax.broadcasted_iota(jnp.int32, (1, LANES), 1)
    sign = jnp.where((lane & (QK_ROPE // 2)) == 0, -1.0, 1.0)
    ca, sa, cb, sb = cosa_ref[0:1, :], sina_ref[0:1, :], cosb_ref[...], sinb_ref[...]
    cos = ca * cb - sa * sb
    sin = (sa * cb + ca * sb) * sign

    ckr = [_dot(_rms(h, nkv[...]).astype(BF16), w_dkv[...]) for h in hs]
    for r, y in zip(groups, ckr):
        c = _rms(y[:, :KV_RANK], kvn[...])
        kr = y[:, KV_RANK:KV_RANK + LANES] * cos[r] + y[:, KV_RANK + LANES:] * sin[r]
        c_ref[0, r, :] = c
        kr_ref[0, r, :] = kr[:, :QK_ROPE]
        ck_ref[0, r, :] = c.astype(ck_ref.dtype)
        krk_ref[0, r, :] = kr.astype(krk_ref.dtype)

    cq = [_rms(_dot(_rms(h, nmix1[...]).astype(BF16), w_dq[...]), qn[...]).astype(BF16) for h in hs]
    q_nope = [_dot(y, w_qn[...]).astype(BF16) for y in cq]
    q_r = [_dot(y, w_qr[...]) for y in cq]
    q_s = [_dot(y, w_qs[...]) for y in cq]
    for k, r in enumerate(groups):
        for hp in range(N_HEADS // 2):
            sl = slice(hp * LANES, (hp + 1) * LANES)
            rot = (q_r[k][:, sl] * cos[r] + q_s[k][:, sl] * sin[r]) * Q_SCALE
            qr_ref[0, 2 * hp, r, :] = jnp.where(lane < QK_ROPE, rot, 0.0).astype(qr_ref.dtype)
            qr_ref[0, 2 * hp + 1, r, :] = jnp.where(lane >= QK_ROPE, rot, 0.0).astype(qr_ref.dtype)
    for hd in range(N_HEADS):
        for k, r in enumerate(groups):
            ql = _dot(q_nope[k][:, hd * QK_NOPE:(hd + 1) * QK_NOPE], w_ukt[hd]) * Q_SCALE
            ql_ref[0, hd, r, :] = ql.astype(ql_ref.dtype)


def _weight_spec(w, layer):
    if layer is None:
        nd = w.ndim
        return pl.BlockSpec(w.shape, lambda *_: (0,) * nd, pipeline_mode=pl.Buffered(1))
    nd = w.ndim - 1
    return pl.BlockSpec((None,) + w.shape[1:], lambda *_: (layer,) + (0,) * nd, pipeline_mode=pl.Buffered(1))


def _front(sample, x_in, p_all, layer, rope, start_pos, wts, tm, n_b, n_t, dec_seq, act_dt):
    n_tok = n_t * tm
    if sample:
        seg = POOL_HALO + dec_seq
        data_specs = [pl.BlockSpec((1, tm // dec_seq * seg, D_MODEL), lambda b, t: (b, t, 0))]
        data = [x_in]
        u_shape, u_spec = (n_b, n_tok, D_MODEL), pl.BlockSpec((1, tm, D_MODEL), lambda b, t: (b, t, 0))
    else:
        hb = tm // POOL_HALO
        data_specs = [pl.BlockSpec((1, tm, D_MODEL), lambda b, t: (b, t, 0)),
                      pl.BlockSpec((1, POOL_HALO, D_MODEL), lambda b, t: (b, jnp.maximum(t * hb - 1, 0), 0))]
        data = [x_in, x_in]
        u_shape, u_spec = (n_b, POOL_HALO, D_MODEL), pl.BlockSpec((1, POOL_HALO, D_MODEL), lambda b, t: (b, 0, 0))
    base_spec = pl.BlockSpec((8, LANES), lambda b, t: (t, 0))
    off_spec = pl.BlockSpec((tm, LANES), lambda b, t: (0, 0), pipeline_mode=pl.Buffered(1))
    data_specs += [pl.BlockSpec((None, None, tm, PLE_DIM), lambda b, t: (layer, b, t, 0)),
                   base_spec, base_spec, off_spec, off_spec]
    data += [p_all, *rope]
    tok = lambda width: pl.BlockSpec((1, tm, width), lambda b, t: (b, t, 0))
    head = lambda width: pl.BlockSpec((1, N_HEADS, tm, width), lambda b, t: (b, 0, t, 0))
    out_shape = [jax.ShapeDtypeStruct((n_b, n_tok, D_MODEL), F32),
                 jax.ShapeDtypeStruct(u_shape, F32),
                 jax.ShapeDtypeStruct((n_b, n_tok, KV_RANK), F32),
                 jax.ShapeDtypeStruct((n_b, n_tok, QK_ROPE), F32),
                 jax.ShapeDtypeStruct((n_b, n_tok, KV_RANK), act_dt),
                 jax.ShapeDtypeStruct((n_b, n_tok, LANES), act_dt),
                 jax.ShapeDtypeStruct((n_b, N_HEADS, n_tok, KV_RANK), act_dt),
                 jax.ShapeDtypeStruct((n_b, N_HEADS, n_tok, LANES), act_dt)]
    out_specs = [tok(D_MODEL), u_spec, tok(KV_RANK), tok(QK_ROPE), tok(KV_RANK), tok(LANES),
                 head(KV_RANK), head(LANES)]
    return pl.pallas_call(
        functools.partial(_front_kernel, sample, tm, start_pos, dec_seq),
        grid=(n_b, n_t),
        in_specs=data_specs + [_weight_spec(w, l) for w, l in wts],
        out_specs=out_specs,
        out_shape=out_shape,
        compiler_params=pltpu.CompilerParams(dimension_semantics=("arbitrary", "arbitrary"),
                                             vmem_limit_bytes=VMEM_LIMIT),
        name="front_sample" if sample else "front_prompt",
    )(*data, *[w for w, _ in wts])


def _back_kernel(tm, h_ref, att_ref, p_ref, w_uv, w_o, nmlp, w_up, w_down, nple, w_gate, w_proj, nfin, y_ref):
    hr = tm // ROW_SPLIT
    groups = [slice(k * hr, (k + 1) * hr) for k in range(ROW_SPLIT)]
    o = [jnp.concatenate([_dot(att_ref[0, hd, r, :].astype(BF16), w_uv[hd]).astype(BF16)
                          for hd in range(N_HEADS)], axis=-1) for r in groups]
    hs = [h_ref[0, r, :] + _dot(y, w_o[...]) for r, y in zip(groups, o)]
    hs = _mlp_ple(hs, [p_ref[r, :] for r in groups], nmlp[...], w_up, w_down, nple[...], w_gate, w_proj)
    for r, h in zip(groups, hs):
        y_ref[0, r, :] = _rms(h, nfin[...])


def _back(h, att, p_all, layer, wts, tm, name):
    n_b, n_tok, _ = h.shape
    return pl.pallas_call(
        functools.partial(_back_kernel, tm),
        grid=(n_b, n_tok // tm),
        in_specs=[pl.BlockSpec((1, tm, D_MODEL), lambda b, t: (b, t, 0)),
                  pl.BlockSpec((1, N_HEADS, tm, KV_RANK), lambda b, t: (b, 0, t, 0)),
                  pl.BlockSpec((None, None, tm, PLE_DIM), lambda b, t: (layer, b, t, 0))]
                 + [_weight_spec(w, l) for w, l in wts],
        out_specs=pl.BlockSpec((1, tm, D_MODEL), lambda b, t: (b, t, 0)),
        out_shape=jax.ShapeDtypeStruct((n_b, n_tok, D_MODEL), F32),
        compiler_params=pltpu.CompilerParams(dimension_semantics=("arbitrary", "arbitrary"),
                                             vmem_limit_bytes=VMEM_LIMIT),
        name=name,
    )(h, att, p_all, *[w for w, _ in wts])


ATTN_PIECES = 8


def _softmax_step(s, v, sl, m_sc, l_sc, acc_sc, first):
    reps = s.shape[1] // LANES
    m_cur = jnp.max(s, axis=1, keepdims=True)
    if first:
        m_new = jnp.broadcast_to(m_cur, (s.shape[0], LANES))
    else:
        m_prev = m_sc[sl, :]
        m_new = jnp.maximum(m_prev, m_cur)
        alpha = jnp.exp2(m_prev - m_new)
    p = jnp.exp2(s - jnp.concatenate([m_new] * reps, axis=1))
    part = p[:, :LANES]
    for r in range(1, reps):
        part = part + p[:, r * LANES:(r + 1) * LANES]
    pv = _dot(p.astype(BF16), v)
    if first:
        l_sc[sl, :] = part
        acc_sc[sl, :] = pv
    else:
        l_sc[sl, :] = alpha * l_sc[sl, :] + part
        acc_sc[sl, :] = acc_sc[sl, :] * jnp.concatenate([alpha] * (KV_RANK // LANES), axis=1) + pv
    m_sc[sl, :] = m_new


def _prompt_attn_kernel(tq, ql_ref, qr_ref, c_ref, kr_ref, o_ref, s_sc, m_sc, l_sc, acc_sc):
    i = pl.program_id(1)
    heads_pp = N_HEADS // ATTN_PIECES
    pr = heads_pp * tq
    pieces = [pl.ds(n * pr, pr) for n in range(ATTN_PIECES)]
    state = (m_sc, l_sc, acc_sc)

    def keys(kc):
        k0 = pl.multiple_of(kc * tq, tq)
        return c_ref[pl.ds(k0, tq), :], kr_ref[pl.ds(k0, tq), :]

    def scores(n, ck, krk):
        hs = slice(n * heads_pp, (n + 1) * heads_pp)
        ql = ql_ref[0, hs].reshape(pr, KV_RANK)
        qr = qr_ref[0, hs].reshape(pr, LANES)
        return _dot_nt(ql, ck) + _dot_nt(qr, krk)

    ck, krk = keys(i)
    for n, sl in enumerate(pieces):
        s_sc[sl, :] = scores(n, ck, krk)

    ck_next, krk_next = keys(0)
    tok = _mod(lax.broadcasted_iota(jnp.int32, (pr, 1), 0), tq)
    key = lax.broadcasted_iota(jnp.int32, (1, tq), 1)
    for n, sl in enumerate(pieces):
        s = jnp.where(key <= tok, s_sc[sl, :], NEG_INF)
        s_sc[sl, :] = scores(n, ck_next, krk_next)
        _softmax_step(s, ck, sl, *state, first=True)

    def full_chunk(kc, carry):
        ck, _ = keys(kc)
        ck_next, krk_next = keys(kc + 1)
        for n, sl in enumerate(pieces):
            s = s_sc[sl, :]
            s_sc[sl, :] = scores(n, ck_next, krk_next)
            _softmax_step(s, ck, sl, *state, first=False)
        return carry

    lax.fori_loop(0, i - 1, full_chunk, 0)

    @pl.when(i > 0)
    def _():
        ck, _ = keys(i - 1)
        for sl in pieces:
            _softmax_step(s_sc[sl, :], ck, sl, *state, first=False)

    out = acc_sc[...] / jnp.sum(l_sc[...], axis=1, keepdims=True)
    o_ref[0] = out.reshape(N_HEADS, tq, KV_RANK).astype(o_ref.dtype)


def _prompt_attn(ql, qr, ck, krk, tq):
    n_b, _, seq, _ = ql.shape
    rows = N_HEADS * tq
    return pl.pallas_call(
        functools.partial(_prompt_attn_kernel, tq),
        grid=(n_b, seq // tq),
        in_specs=[pl.BlockSpec((1, N_HEADS, tq, KV_RANK), lambda b, i: (b, 0, i, 0)),
                  pl.BlockSpec((1, N_HEADS, tq, LANES), lambda b, i: (b, 0, i, 0)),
                  pl.BlockSpec((None, seq, KV_RANK), lambda b, i: (b, 0, 0)),
                  pl.BlockSpec((None, seq, LANES), lambda b, i: (b, 0, 0))],
        out_specs=pl.BlockSpec((1, N_HEADS, tq, KV_RANK), lambda b, i: (b, 0, i, 0)),
        out_shape=jax.ShapeDtypeStruct(ql.shape, BF16),
        scratch_shapes=[pltpu.VMEM((rows, tq), F32), pltpu.VMEM((rows, LANES), F32),
                        pltpu.VMEM((rows, LANES), F32), pltpu.VMEM((rows, KV_RANK), F32)],
        compiler_params=pltpu.CompilerParams(dimension_semantics=("arbitrary", "arbitrary"),
                                             vmem_limit_bytes=VMEM_LIMIT),
        name="attend_prompt",
    )(ql, qr, ck, krk)


SAMPLE_CHUNK_PAGES = 16
SAMPLE_SLOTS = 3


def _sample_attn_kernel(n_pages, dec_seq, pt_ref, ql_ref, qr_ref, cn_ref, krn_ref, cache_c, cache_krt, o_ref,
                        kc_buf, krt_buf, sem):
    b = pl.program_id(0)
    n_b = pl.num_programs(0)
    slot = b % SAMPLE_SLOTS
    rows = N_HEADS * dec_seq

    def page_copies(bb, sl, j):
        pg = pt_ref[bb, j]
        return (pltpu.make_async_copy(cache_c.at[pg], kc_buf.at[sl, j], sem.at[0, sl]),
                pltpu.make_async_copy(cache_krt.at[pg], krt_buf.at[sl, :, pl.ds(j * PAGE_SIZE, PAGE_SIZE)],
                                      sem.at[1, sl]))

    def start_fetch(bb, sl):
        for j in range(n_pages):
            for cp in page_copies(bb, sl, j):
                cp.start()

    ahead = SAMPLE_SLOTS - 1

    @pl.when(b == 0)
    def _():
        for bb in range(ahead):
            @pl.when(bb < n_b)
            def _():
                start_fetch(bb, bb)

    @pl.when(b + ahead < n_b)
    def _():
        start_fetch(b + ahead, (b + ahead) % SAMPLE_SLOTS)

    for j in range(n_pages):
        for cp in page_copies(b, slot, j):
            cp.wait()

    ql = ql_ref[...].reshape(rows, KV_RANK)
    qr2 = qr_ref[...].reshape(rows, LANES)
    qr = qr2[:, :QK_ROPE] + qr2[:, QK_ROPE:]

    def block(s, v):
        m = jnp.max(s, axis=1, keepdims=True)
        p = jnp.exp2(s - m)
        return m, jnp.sum(p, axis=1, keepdims=True), _dot(p, v)

    ck_keys = SAMPLE_CHUNK_PAGES * PAGE_SIZE
    n_chunks = n_pages // SAMPLE_CHUNK_PAGES
    values = [kc_buf[slot, pl.ds(j * SAMPLE_CHUNK_PAGES, SAMPLE_CHUNK_PAGES)].reshape(ck_keys, KV_RANK)
              for j in range(n_chunks)]
    scores = [_dot_nt(ql, values[j]) + _dot(qr, krt_buf[slot, :, pl.ds(j * ck_keys, ck_keys)])
              for j in range(n_chunks)]
    parts = [block(s, v) for s, v in zip(scores, values)]
    cn = jnp.concatenate([cn_ref[0], jnp.zeros((PAGE_SIZE - dec_seq, KV_RANK), F32)], axis=0)
    krn = jnp.concatenate([krn_ref[0], jnp.zeros((PAGE_SIZE - dec_seq, LANES), F32)], axis=0)
    s = _dot_nt(ql, cn) + _dot_nt(qr2, krn)
    tok = _mod(lax.broadcasted_iota(jnp.int32, (rows, 1), 0), dec_seq)
    key = lax.broadcasted_iota(jnp.int32, (1, PAGE_SIZE), 1)
    parts.append(block(jnp.where(key <= tok, s, NEG_INF), cn))

    m_all = functools.reduce(jnp.maximum, [m for m, _, _ in parts])
    num = jnp.zeros((rows, KV_RANK), F32)
    den = jnp.zeros((rows, 1), F32)
    for m, l, o in parts:
        w = jnp.exp2(m - m_all)
        num = num + w * o
        den = den + w * l
    o_ref[...] = (num / den).reshape(N_HEADS, dec_seq, KV_RANK)


def _sample_attn(page_table, ql, qr, c_new, kr_new, cache_c, cache_krt):
    n_h, n_b, dec_seq, _ = ql.shape
    n_pages = page_table.shape[1]
    assert n_pages % SAMPLE_CHUNK_PAGES == 0
    grid_spec = pltpu.PrefetchScalarGridSpec(
        num_scalar_prefetch=1,
        grid=(n_b,),
        in_specs=[pl.BlockSpec((n_h, None, dec_seq, KV_RANK), lambda b, pt: (0, b, 0, 0)),
                  pl.BlockSpec((n_h, None, dec_seq, LANES), lambda b, pt: (0, b, 0, 0)),
                  pl.BlockSpec((1, dec_seq, KV_RANK), lambda b, pt: (b, 0, 0)),
                  pl.BlockSpec((1, dec_seq, LANES), lambda b, pt: (b, 0, 0)),
                  pl.BlockSpec(memory_space=pl.ANY),
                  pl.BlockSpec(memory_space=pl.ANY)],
        out_specs=pl.BlockSpec((n_h, None, dec_seq, KV_RANK), lambda b, pt: (0, b, 0, 0)),
        scratch_shapes=[pltpu.VMEM((SAMPLE_SLOTS, n_pages, PAGE_SIZE, KV_RANK), F32),
                        pltpu.VMEM((SAMPLE_SLOTS, QK_ROPE, n_pages * PAGE_SIZE), F32),
                        pltpu.SemaphoreType.DMA((2, SAMPLE_SLOTS))])
    return pl.pallas_call(
        functools.partial(_sample_attn_kernel, n_pages, dec_seq),
        grid_spec=grid_spec,
        out_shape=jax.ShapeDtypeStruct(ql.shape, F32),
        compiler_params=pltpu.CompilerParams(dimension_semantics=("arbitrary",),
                                             vmem_limit_bytes=VMEM_LIMIT),
        name="attend_sample",
    )(page_table, ql, qr, c_new, kr_new, cache_c, cache_krt)


def _rope_angles(pos, rows_each):
    half = QK_ROPE // 2
    inv_freq = jnp.power(ROPE_THETA, -jnp.arange(half, dtype=F32) / half)
    ang = jnp.repeat(pos.astype(F32), rows_each)[:, None] * inv_freq[None, :]
    reps = LANES // half
    return jnp.concatenate([jnp.cos(ang)] * reps, axis=1), jnp.concatenate([jnp.sin(ang)] * reps, axis=1)


def _swap_halves(w):
    half = w.shape[-1] // 2
    return jnp.concatenate([w[..., half:], w[..., :half]], axis=-1)


PROMPT_TILE = 512
PROMPT_ATTN_TILE = 512
SAMPLE_TILE = 256


def kernel(x_prompt, x_sample, p_prompt, p_sample, state_pool, cache_latent, cache_krope, page_table, norm_mix, norm_mlp, norm_ple, pool_w, pool_scale, norm_kv, w_dkv, kv_norm, w_uk, w_uv, w_dq, q_norm, w_uq, w_o, w_up, w_down, w_ple_gate, w_ple_proj, norm_final):
    n_b, seq, _ = x_prompt.shape
    dec_b, dec_seq, _ = x_sample.shape
    past_len = page_table.shape[1] * PAGE_SIZE
    row = lambda v: (v.reshape(1, -1), None)
    bf = lambda w: w.astype(BF16)

    w_kr = w_dkv[:, KV_RANK:]
    w_dkv_x = bf(jnp.concatenate([w_dkv[:, :KV_RANK], w_kr, w_kr, _swap_halves(w_kr), _swap_halves(w_kr)], axis=1))
    w_qn = bf(w_uq[0, :, :, :QK_NOPE].reshape(Q_RANK, N_HEADS * QK_NOPE))
    w_qr = bf(w_uq[0, :, :, QK_NOPE:].reshape(Q_RANK, N_HEADS * QK_ROPE))
    w_qs = bf(_swap_halves(w_uq[0, :, :, QK_NOPE:]).reshape(Q_RANK, N_HEADS * QK_ROPE))
    w_ukt = bf(jnp.transpose(w_uk, (1, 2, 0)))
    w_up_b, w_down_b, w_gate_b, w_proj_b = bf(w_up), bf(w_down), bf(w_ple_gate), bf(w_ple_proj)
    front_w = [row(norm_mix[0]), row(norm_mlp[0]), row(norm_ple[0]), row(pool_scale[0]), (bf(pool_w[0]), None),
               (w_up_b, 0), (w_down_b, 0), (w_gate_b, 0), (w_proj_b, 0),
               row(norm_kv), (w_dkv_x, None), row(kv_norm), row(norm_mix[1]), (bf(w_dq[0]), None), row(q_norm[0]),
               (w_qn, None), (w_qr, None), (w_qs, None), (w_ukt, None)]
    back_w = [(bf(jnp.transpose(w_uv, (1, 0, 2))), None), (bf(w_o[0].reshape(N_HEADS * V_DIM, D_MODEL)), None),
              row(norm_mlp[1]), (w_up_b, 1), (w_down_b, 1), row(norm_ple[1]), (w_gate_b, 1), (w_proj_b, 1),
              row(norm_final)]

    tm = PROMPT_TILE
    n_t = seq // tm
    rope_p = (*_rope_angles(jnp.arange(n_t) * tm, 8), *_rope_angles(jnp.arange(tm), 1))
    h1, u_tail, lat_p, kr_p, ck, krk, ql, qr = _front(
        False, x_prompt, p_prompt, 0, rope_p, 0, front_w, tm, n_b, n_t, None, BF16)
    att = _prompt_attn(ql, qr, ck, krk, PROMPT_ATTN_TILE)
    y_prompt = _back(h1, att, p_prompt, 1, back_w, tm, "back_prompt")
    pool_prompt = u_tail[None, :, 1:, :]

    n_tok = dec_b * dec_seq
    tms = SAMPLE_TILE
    n_ts = n_tok // tms
    ext = jnp.concatenate([jnp.zeros((dec_b, 1, D_MODEL), F32), state_pool[0], x_sample], axis=1)
    ext = ext.reshape(1, dec_b * (POOL_HALO + dec_seq), D_MODEL)
    rope_s = (*_rope_angles(jnp.full((n_ts,), past_len), 8), *_rope_angles(jnp.arange(tms) % dec_seq, 1))
    p_s = p_sample.reshape(p_sample.shape[0], 1, n_tok, PLE_DIM)
    h1s, u_s, lat_s, kr_s, cks, krks, qls, qrs = _front(
        True, ext, p_s, 0, rope_s, past_len, front_w, tms, 1, n_ts, dec_seq, F32)
    att_s = _sample_attn(page_table,
                         qls.reshape(N_HEADS, dec_b, dec_seq, KV_RANK), qrs.reshape(N_HEADS, dec_b, dec_seq, LANES),
                         cks.reshape(dec_b, dec_seq, KV_RANK), krks.reshape(dec_b, dec_seq, LANES),
                         cache_latent, jnp.transpose(cache_krope, (0, 2, 1)))
    y_sample = _back(h1s, att_s.reshape(1, N_HEADS, n_tok, KV_RANK), p_s, 1, back_w, tms, "back_sample")
    pool_sample = jnp.concatenate([state_pool[0][:, dec_seq:, :], u_s.reshape(dec_b, dec_seq, D_MODEL)], axis=1)[None]

    return (y_prompt, y_sample.reshape(dec_b, dec_seq, D_MODEL), pool_prompt, pool_sample,
            lat_p, kr_p, lat_s.reshape(dec_b, dec_seq, KV_RANK), kr_s.reshape(dec_b, dec_seq, QK_ROPE))
```
